```python
import math
import jax, jax.numpy as jnp
from jax import lax
import numpy as np

D_MODEL = 1024
BATCH = 8
SEQ = 4096
DEPTH = 1
DEC_BATCH = 16
DEC_SEQ = 4096
PAST_LEN = 128

MIX_WIDTH = D_MODEL
LRU_WIDTH = MIX_WIDTH // 2
POOL_WIDTH = MIX_WIDTH - LRU_WIDTH
LRU_HEADS = 8
LRU_HEAD_DIM = LRU_WIDTH // LRU_HEADS
CONV_WIDTH = 4
LRU_C = 8.0
POOL_WINDOWS = (2, 4, 8, 16)
POOL_GROUPS = len(POOL_WINDOWS)
POOL_GROUP_DIM = POOL_WIDTH // POOL_GROUPS
IN_WIDTH = 2 * LRU_WIDTH + POOL_WIDTH
D_FF = 2816
FFN_RES = 0.5
EPS = 1e-6

kernel_name = "hybrid_rglru_pool_macaron_encoder"


def rmsnorm(x, g):
    xf = x.astype(jnp.float32)
    ms = jnp.mean(xf * xf, axis=-1, keepdims=True)
    return (xf * lax.rsqrt(ms + EPS)).astype(x.dtype) * g


def swiglu(h, w_in, w_out):
    gu = h @ w_in
    g, u = jnp.split(gu, 2, axis=-1)
    return (jax.nn.silu(g) * u) @ w_out


def macaron_ffn(x, pre_g, post_g, w_in, w_out):
    y = swiglu(rmsnorm(x, pre_g), w_in, w_out)
    return x + FFN_RES * rmsnorm(y, post_g)


def centred_depthwise_conv(x, w, b):
    S = x.shape[1]
    left = CONV_WIDTH // 2
    xp = jnp.pad(x, ((0, 0), (left, CONV_WIDTH - 1 - left), (0, 0)))
    out = b.astype(jnp.float32)
    for k in range(CONV_WIDTH):
        out = out + xp[:, k:k + S, :] * w[k].astype(jnp.float32)
    return out


def _lin_combine(e1, e2):
    a1, b1 = e1
    a2, b2 = e2
    return a1 * a2, a2 * b1 + b2


def rglru_direction(xc, w_a, b_a, w_x, b_x, lam, reverse):
    B, S, R = xc.shape
    xh = xc.reshape(B, S, LRU_HEADS, LRU_HEAD_DIM)
    r = jax.nn.sigmoid(jnp.einsum('bshi,hij->bshj', xh, w_a.astype(jnp.float32)).reshape(B, S, R)
                       + b_a.astype(jnp.float32))
    i = jax.nn.sigmoid(jnp.einsum('bshi,hij->bshj', xh, w_x.astype(jnp.float32)).reshape(B, S, R)
                       + b_x.astype(jnp.float32))
    log_a = -LRU_C * r * jax.nn.softplus(-lam.astype(jnp.float32))
    a = jnp.exp(log_a)
    u = jnp.sqrt(-jnp.expm1(2.0 * log_a)) * (i * xc)
    _, h = lax.associative_scan(_lin_combine, (a, u), reverse=reverse, axis=1)
    return h


def pool_mixer(p, w_pool, scale):
    B, S, _ = p.shape
    pf = p.astype(jnp.float32)
    cs = jnp.concatenate([jnp.zeros((B, 1, POOL_WIDTH), jnp.float32), jnp.cumsum(pf, axis=1)], axis=1)
    t = jnp.arange(S)
    outs = []
    for g, w in enumerate(POOL_WINDOWS):
        sl = slice(g * POOL_GROUP_DIM, (g + 1) * POOL_GROUP_DIM)
        lo = jnp.clip(t - w // 2, 0, S)
        hi = jnp.clip(t + w // 2, 0, S)
        csg = cs[..., sl]
        cnt = (hi - lo).astype(jnp.float32)[None, :, None]
        mean = (jnp.take(csg, hi, axis=1) - jnp.take(csg, lo, axis=1)) / cnt
        d = (mean - pf[..., sl]).astype(p.dtype)
        outs.append(d @ w_pool[g])
    return jnp.concatenate(outs, axis=-1) * scale


def token_mixing(x, pre_g, post_g, w_in, conv_w, conv_b, lru_w_a, lru_b_a, lru_w_x, lru_b_x,
                 lru_lam, lru_out_g, pool_w, pool_scale, pool_out_g, w_out):
    h = rmsnorm(x, pre_g)
    z = h @ w_in
    xb = z[..., :LRU_WIDTH]
    gb = z[..., LRU_WIDTH:2 * LRU_WIDTH]
    pb = z[..., 2 * LRU_WIDTH:]
    xc = centred_depthwise_conv(xb.astype(jnp.float32), conv_w, conv_b)
    h_f = rglru_direction(xc, lru_w_a[0], lru_b_a[0], lru_w_x[0], lru_b_x[0], lru_lam[0], False)
    h_b = rglru_direction(xc, lru_w_a[1], lru_b_a[1], lru_w_x[1], lru_b_x[1], lru_lam[1], True)
    lru = (h_f + h_b).astype(x.dtype) * jax.nn.gelu(gb)
    lru = rmsnorm(lru, lru_out_g)
    pool = rmsnorm(pool_mixer(pb, pool_w, pool_scale), pool_out_g)
    o = jnp.concatenate([lru, pool], axis=-1) @ w_out
    return x + rmsnorm(o, post_g)


def encoder_layer(x, l, ffn1_pre_g, ffn1_post_g, ffn1_w_in, ffn1_w_out,
                  mix_pre_g, mix_post_g, w_in, conv_w, conv_b, lru_w_a, lru_b_a, lru_w_x, lru_b_x,
                  lru_lam, lru_out_g, pool_w, pool_scale, pool_out_g, w_out,
                  ffn2_pre_g, ffn2_post_g, ffn2_w_in, ffn2_w_out):
    x = macaron_ffn(x, ffn1_pre_g[l], ffn1_post_g[l], ffn1_w_in[l], ffn1_w_out[l])
    x = token_mixing(x, mix_pre_g[l], mix_post_g[l], w_in[l], conv_w[l], conv_b[l],
                     lru_w_a[l], lru_b_a[l], lru_w_x[l], lru_b_x[l], lru_lam[l], lru_out_g[l],
                     pool_w[l], pool_scale[l], pool_out_g[l], w_out[l])
    x = macaron_ffn(x, ffn2_pre_g[l], ffn2_post_g[l], ffn2_w_in[l], ffn2_w_out[l])
    return x


def setup_inputs(seed: int = 0) -> dict:
    key = jax.random.key(seed)
    ks = jax.random.split(key, 32)
    f32 = jnp.float32
    nrm = lambda k, shape, fan_in: jax.random.normal(k, shape, f32) * (fan_in ** -0.5)
    gain = lambda k, n: 1.0 + 0.05 * jax.random.normal(k, (DEPTH, n), f32)
    a_c = jax.random.uniform(ks[12], (DEPTH, 2, LRU_WIDTH), f32, 0.9, 0.999)
    p0 = a_c ** (1.0 / LRU_C)
    lam = jnp.log(p0) - jnp.log1p(-p0)
    return {
        "x_prompt": jax.random.normal(ks[0], (BATCH, SEQ, D_MODEL), f32),
        "x_sample": jax.random.normal(ks[1], (DEC_BATCH, DEC_SEQ, D_MODEL), f32),
        "ffn1_pre_g": gain(ks[2], D_MODEL),
        "ffn1_post_g": gain(ks[3], D_MODEL),
        "ffn1_w_in": nrm(ks[4], (DEPTH, D_MODEL, 2 * D_FF), D_MODEL),
        "ffn1_w_out": nrm(ks[5], (DEPTH, D_FF, D_MODEL), D_FF),
        "mix_pre_g": gain(ks[6], D_MODEL),
        "mix_post_g": gain(ks[7], D_MODEL),
        "w_in": nrm(ks[8], (DEPTH, D_MODEL, IN_WIDTH), D_MODEL),
        "conv_w": nrm(ks[9], (DEPTH, CONV_WIDTH, LRU_WIDTH), CONV_WIDTH),
        "conv_b": 0.02 * jax.random.normal(ks[10], (DEPTH, LRU_WIDTH), f32),
        "lru_w_a": nrm(ks[11], (DEPTH, 2, LRU_HEADS, LRU_HEAD_DIM, LRU_HEAD_DIM), LRU_HEAD_DIM),
        "lru_b_a": 0.02 * jax.random.normal(ks[13], (DEPTH, 2, LRU_WIDTH), f32),
        "lru_w_x": nrm(ks[14], (DEPTH, 2, LRU_HEADS, LRU_HEAD_DIM, LRU_HEAD_DIM), LRU_HEAD_DIM),
        "lru_b_x": 0.02 * jax.random.normal(ks[15], (DEPTH, 2, LRU_WIDTH), f32),
        "lru_lam": lam,
        "lru_out_g": gain(ks[16], LRU_WIDTH),
        "pool_w": nrm(ks[17], (DEPTH, POOL_GROUPS, POOL_GROUP_DIM, POOL_GROUP_DIM), POOL_GROUP_DIM),
        "pool_scale": 1.0 + 0.1 * jax.random.normal(ks[18], (DEPTH, POOL_WIDTH), f32),
        "pool_out_g": gain(ks[19], POOL_WIDTH),
        "w_out": nrm(ks[20], (DEPTH, MIX_WIDTH, D_MODEL), MIX_WIDTH),
        "ffn2_pre_g": gain(ks[21], D_MODEL),
        "ffn2_post_g": gain(ks[22], D_MODEL),
        "ffn2_w_in": nrm(ks[23], (DEPTH, D_MODEL, 2 * D_FF), D_MODEL),
        "ffn2_w_out": nrm(ks[24], (DEPTH, D_FF, D_MODEL), D_FF),
    }


def reference(x_prompt, x_sample, ffn1_pre_g, ffn1_post_g, ffn1_w_in, ffn1_w_out,
              mix_pre_g, mix_post_g, w_in, conv_w, conv_b, lru_w_a, lru_b_a, lru_w_x, lru_b_x,
              lru_lam, lru_out_g, pool_w, pool_scale, pool_out_g, w_out,
              ffn2_pre_g, ffn2_post_g, ffn2_w_in, ffn2_w_out):
    y_prompt = x_prompt
    y_sample = x_sample
    for l in range(DEPTH):
        y_prompt = encoder_layer(y_prompt, l, ffn1_pre_g, ffn1_post_g, ffn1_w_in, ffn1_w_out,
                                 mix_pre_g, mix_post_g, w_in, conv_w, conv_b, lru_w_a, lru_b_a,
                                 lru_w_x, lru_b_x, lru_lam, lru_out_g, pool_w, pool_scale,
                                 pool_out_g, w_out, ffn2_pre_g, ffn2_post_g, ffn2_w_in, ffn2_w_out)
        y_sample = encoder_layer(y_sample, l, ffn1_pre_g, ffn1_post_g, ffn1_w_in, ffn1_w_out,
                                 mix_pre_g, mix_post_g, w_in, conv_w, conv_b, lru_w_a, lru_b_a,
                                 lru_w_x, lru_b_x, lru_lam, lru_out_g, pool_w, pool_scale,
                                 pool_out_g, w_out, ffn2_pre_g, ffn2_post_g, ffn2_w_in, ffn2_w_out)
    return (y_prompt, y_sample)
```

```python
import functools

import jax
import jax.numpy as jnp
from jax import lax
from jax.experimental import pallas as pl
from jax.experimental.pallas import tpu as pltpu

D_MODEL = 1024
LRU_WIDTH = 512
POOL_WIDTH = 512
LRU_HEADS = 8
LRU_HEAD_DIM = 64
CONV_WIDTH = 4
CONV_LEFT = CONV_WIDTH // 2
LRU_C = 8.0
POOL_WINDOWS = (2, 4, 8, 16)
POOL_GROUP_DIM = 128
IN_WIDTH = 2 * LRU_WIDTH + POOL_WIDTH
D_FF = 2816
FFN_RES = 0.5
EPS = 1e-6

LANES = 128
SUBLANES = 8
MXU_DIM = 256
SEQ_GROUP = SUBLANES

FFN_ROWS = 512
FF_CHUNK = MXU_DIM
N_FF_CHUNKS = D_FF // FF_CHUNK

MIX_T = 64
HALO_L = 8
HALO_R = 16
MIX_P = HALO_L + MIX_T + HALO_R

VMEM_LIMIT = 52 * 1024 * 1024

_F32 = jnp.float32
_BF16 = jnp.bfloat16


def _rms(v, g):
    ms = jnp.mean(v * v, axis=-1, keepdims=True)
    return (v * lax.rsqrt(ms + EPS)) * g


def _ffn_kernel(x_ref, pre_g_ref, post_g_ref, w_in_ref, w_out_ref, o_ref):
    x = x_ref[...]
    h = _rms(x, pre_g_ref[...]).astype(_BF16)
    acc = None
    for c in range(N_FF_CHUNKS):
        gu = jnp.dot(h, w_in_ref[c], preferred_element_type=_F32)
        act = (jax.nn.silu(gu[:, :FF_CHUNK]) * gu[:, FF_CHUNK:]).astype(_BF16)
        part = jnp.dot(act, w_out_ref[c], preferred_element_type=_F32)
        acc = part if acc is None else acc + part
    o_ref[...] = x + FFN_RES * _rms(acc, post_g_ref[...])


def _ffn(x2d, pre_g, post_g, w_in_r, w_out_r):
    n_rows = x2d.shape[0]
    rows = min(FFN_ROWS, n_rows)
    assert n_rows % rows == 0
    const2 = lambda i: (0, 0)
    const3 = lambda i: (0, 0, 0)
    return pl.pallas_call(
        _ffn_kernel,
        out_shape=jax.ShapeDtypeStruct(x2d.shape, x2d.dtype),
        grid=(n_rows // rows,),
        in_specs=[
            pl.BlockSpec((rows, D_MODEL), lambda i: (i, 0)),
            pl.BlockSpec((1, D_MODEL), const2),
            pl.BlockSpec((1, D_MODEL), const2),
            pl.BlockSpec(w_in_r.shape, const3, pipeline_mode=pl.Buffered(1)),
            pl.BlockSpec(w_out_r.shape, const3, pipeline_mode=pl.Buffered(1)),
        ],
        out_specs=pl.BlockSpec((rows, D_MODEL), lambda i: (i, 0)),
        compiler_params=pltpu.CompilerParams(
            dimension_semantics=("arbitrary",), vmem_limit_bytes=VMEM_LIMIT),
        name="macaron_ffn",
    )(x2d, pre_g, post_g, w_in_r, w_out_r)


def _lru_gates(xc, gate_w_ref, b_a_ref, b_x_ref, lam_ref, a_s, u_s):
    xcb = xc.astype(_BF16)
    clam = -LRU_C * jax.nn.softplus(-lam_ref[...])
    for jg in range(LRU_WIDTH // MXU_DIM):
        sl = slice(jg * MXU_DIM, (jg + 1) * MXU_DIM)
        g = jnp.dot(xcb[:, sl], gate_w_ref[jg], preferred_element_type=_F32)
        r = jax.nn.sigmoid(g[:, :MXU_DIM] + b_a_ref[:, sl])
        i = jax.nn.sigmoid(g[:, MXU_DIM:] + b_x_ref[:, sl])
        log_a = r * clam[:, sl]
        a = jnp.exp(log_a)
        mult = jnp.sqrt(-jnp.tanh(log_a) * (a * a + 1.0))
        a_s[:, sl] = a
        u_s[:, sl] = mult * (i * xc[:, sl])


def _mix1_kernel(xm_ref, xp_ref, xn_ref, pre_g_ref, w_in_ref, conv_w_ref, conv_b_ref,
                 gate_w_ref, b_a_ref, b_x_ref, lam_ref, pool_w_ref, pool_scale_ref,
                 pool_g_ref,
                 xc_ref, hf_ref, gb_ref, pool_ref,
                 hn_s, z_s, zp_s, a_s, u_s, carry_s, *, seq_len):
    T, P = MIX_T, MIX_P
    c = pl.program_id(1)
    n_c = pl.num_programs(1)
    pre_g = pre_g_ref[...]

    hn_s[:, 0:HALO_L, :] = jnp.where(c > 0, _rms(xp_ref[...], pre_g), 0.0)
    hn_s[:, HALO_L:HALO_L + T, :] = _rms(xm_ref[...], pre_g)
    hn_s[:, HALO_L + T:P, :] = jnp.where(c < n_c - 1, _rms(xn_ref[...], pre_g), 0.0)

    h2 = hn_s[...].reshape(SEQ_GROUP * P, D_MODEL).astype(_BF16)
    z = jnp.dot(h2, w_in_ref[...], preferred_element_type=_F32)
    n_tiles = IN_WIDTH // LANES
    for n in range(n_tiles):
        z_s[n] = z[:, n * LANES:(n + 1) * LANES]

    def permute(j, carry):
        dst = pl.ds(pl.multiple_of(j * SUBLANES, SUBLANES), SUBLANES)
        for n in range(n_tiles):
            zp_s[n, dst, :] = z_s[n, pl.ds(j, SEQ_GROUP, stride=P), :]
        return carry
    lax.fori_loop(0, P, permute, 0, unroll=4)

    R = SUBLANES
    main = slice(HALO_L * R, (HALO_L + T) * R)
    lru_tiles = LRU_WIDTH // LANES

    xc_tiles = []
    for n in range(lru_tiles):
        ls = slice(n * LANES, (n + 1) * LANES)
        acc = conv_b_ref[:, ls]
        for k in range(CONV_WIDTH):
            lo = (HALO_L + k - CONV_LEFT) * R
            acc = acc + zp_s[n, lo:lo + T * R, :] * conv_w_ref[k:k + 1, ls]
        xc_tiles.append(acc)
    xc = jnp.concatenate(xc_tiles, axis=-1)
    xc_ref[...] = xc
    gb_ref[...] = jnp.concatenate(
        [zp_s[lru_tiles + n, main, :] for n in range(lru_tiles)], axis=-1)

    _lru_gates(xc, gate_w_ref, b_a_ref, b_x_ref, lam_ref, a_s, u_s)
    h = jnp.where(c > 0, carry_s[...], 0.0)
    for j in range(T):
        rows = slice(j * R, (j + 1) * R)
        h = a_s[rows, :] * h + u_s[rows, :]
        hf_ref[rows, :] = h
    carry_s[...] = h

    t = c * T + lax.broadcasted_iota(jnp.int32, (T * R, LANES), 0) // R
    d_tiles = []
    for gi, w in enumerate(POOL_WINDOWS):
        n = 2 * lru_tiles + gi

        def win(m, start, length):
            if m == 1:
                return zp_s[n, start * R:(start + length) * R, :]
            half = win(m // 2, start - m // 2, length + m // 2)
            return half[(m // 2) * R:, :] + half[:length * R, :]

        total = win(w, HALO_L + w // 2 - 1, T)
        cnt = (jnp.minimum(t + w // 2, seq_len) - jnp.maximum(t - w // 2, 0)).astype(_F32)
        d_tiles.append(total / cnt - zp_s[n, main, :])
    d = jnp.concatenate(d_tiles, axis=-1).astype(_BF16)
    pooled = jnp.concatenate(
        [jnp.dot(d[:, q * MXU_DIM:(q + 1) * MXU_DIM], pool_w_ref[q],
                 preferred_element_type=_F32) for q in range(POOL_WIDTH // MXU_DIM)],
        axis=-1) * pool_scale_ref[...]
    pool_ref[...] = _rms(pooled, pool_g_ref[...]).astype(_BF16)


def _mix1(x4, pre_g, w_in_b, conv_w, conv_b, gate_w, b_a, b_x, lam, pool_w, pool_scale, pool_g):
    G, _, S, _ = x4.shape
    T, P = MIX_T, MIX_P
    assert S % T == 0 and T % HALO_R == 0
    n_c = S // T
    rows = T * SUBLANES
    c2 = lambda g, c: (0, 0)
    c3 = lambda g, c: (0, 0, 0)
    perm_spec = pl.BlockSpec((None, rows, LRU_WIDTH), lambda g, c: (g, c, 0))
    perm_shape = (G, S * SUBLANES, LRU_WIDTH)
    return pl.pallas_call(
        functools.partial(_mix1_kernel, seq_len=S),
        out_shape=[jax.ShapeDtypeStruct(perm_shape, _F32),
                   jax.ShapeDtypeStruct(perm_shape, _F32),
                   jax.ShapeDtypeStruct(perm_shape, _F32),
                   jax.ShapeDtypeStruct(perm_shape, _BF16)],
        grid=(G, n_c),
        in_specs=[
            pl.BlockSpec((None, SEQ_GROUP, T, D_MODEL), lambda g, c: (g, 0, c, 0)),
            pl.BlockSpec((None, SEQ_GROUP, HALO_L, D_MODEL),
                         lambda g, c: (g, 0, jnp.maximum(c * (T // HALO_L) - 1, 0), 0)),
            pl.BlockSpec((None, SEQ_GROUP, HALO_R, D_MODEL),
                         lambda g, c: (g, 0, jnp.minimum((c + 1) * (T // HALO_R),
                                                         S // HALO_R - 1), 0)),
            pl.BlockSpec((1, D_MODEL), c2),
            pl.BlockSpec((D_MODEL, IN_WIDTH), c2, pipeline_mode=pl.Buffered(1)),
            pl.BlockSpec((CONV_WIDTH, LRU_WIDTH), c2),
            pl.BlockSpec((1, LRU_WIDTH), c2),
            pl.BlockSpec(gate_w.shape, c3, pipeline_mode=pl.Buffered(1)),
            pl.BlockSpec((1, LRU_WIDTH), c2),
            pl.BlockSpec((1, LRU_WIDTH), c2),
            pl.BlockSpec((1, LRU_WIDTH), c2),
            pl.BlockSpec(pool_w.shape, c3, pipeline_mode=pl.Buffered(1)),
            pl.BlockSpec((1, POOL_WIDTH), c2),
            pl.BlockSpec((1, POOL_WIDTH), c2),
        ],
        out_specs=[perm_spec, perm_spec, perm_spec, perm_spec],
        scratch_shapes=[
            pltpu.VMEM((SEQ_GROUP, P, D_MODEL), _F32),
            pltpu.VMEM((IN_WIDTH // LANES, SEQ_GROUP * P, LANES), _F32),
            pltpu.VMEM((IN_WIDTH // LANES, P * SUBLANES, LANES), _F32),
            pltpu.VMEM((rows, LRU_WIDTH), _F32),
            pltpu.VMEM((rows, LRU_WIDTH), _F32),
            pltpu.VMEM((SUBLANES, LRU_WIDTH), _F32),
        ],
        compiler_params=pltpu.CompilerParams(
            dimension_semantics=("arbitrary", "arbitrary"), vmem_limit_bytes=VMEM_LIMIT),
        name="mix_forward",
    )(x4, x4, x4, pre_g, w_in_b, conv_w, conv_b, gate_w, b_a, b_x, lam, pool_w, pool_scale,
      pool_g)


def _mix2_kernel(xc_ref, hf_ref, gb_ref, pool_ref, x_ref, gate_w_ref, b_a_ref, b_x_ref,
                 lam_ref, lru_g_ref, w_out_ref, post_g_ref,
                 y_ref,
                 a_s, u_s, hb_s, carry_s, y_s):
    T = MIX_T
    R = SUBLANES
    c = pl.program_id(1)

    _lru_gates(xc_ref[...], gate_w_ref, b_a_ref, b_x_ref, lam_ref, a_s, u_s)
    h = jnp.where(c > 0, carry_s[...], 0.0)
    for j in reversed(range(T)):
        rows = slice(j * R, (j + 1) * R)
        h = a_s[rows, :] * h + u_s[rows, :]
        hb_s[rows, :] = h
    carry_s[...] = h

    lru = (hf_ref[...] + hb_s[...]) * jax.nn.gelu(gb_ref[...])
    lru_n = _rms(lru, lru_g_ref[...]).astype(_BF16)
    cat = jnp.concatenate([lru_n, pool_ref[...]], axis=-1)
    o = jnp.dot(cat, w_out_ref[...], preferred_element_type=_F32)
    yp = _rms(o, post_g_ref[...])
    d_tiles = D_MODEL // LANES
    for n in range(d_tiles):
        y_s[n] = yp[:, n * LANES:(n + 1) * LANES]

    def unpermute(i, carry):
        s = i // (T // R)
        jb = i % (T // R)
        t0 = pl.multiple_of(jb * R, R)
        for n in range(d_tiles):
            v = y_s[n, pl.ds(jb * (R * R) + s, R, stride=R), :]
            ls = slice(n * LANES, (n + 1) * LANES)
            y_ref[s, pl.ds(t0, R), ls] = x_ref[s, pl.ds(t0, R), ls] + v
        return carry
    lax.fori_loop(0, SEQ_GROUP * (T // R), unpermute, 0, unroll=2)


def _mix2(xc_p, hf_p, gb_p, pool_p, x4, gate_w, b_a, b_x, lam, lru_g, w_out_b, post_g):
    G, _, S, _ = x4.shape
    T = MIX_T
    n_c = S // T
    rows = T * SUBLANES
    c2 = lambda g, c: (0, 0)
    c3 = lambda g, c: (0, 0, 0)
    perm_spec = pl.BlockSpec((None, rows, LRU_WIDTH), lambda g, c: (g, n_c - 1 - c, 0))
    x_spec = pl.BlockSpec((None, SEQ_GROUP, T, D_MODEL), lambda g, c: (g, 0, n_c - 1 - c, 0))
    return pl.pallas_call(
        _mix2_kernel,
        out_shape=jax.ShapeDtypeStruct(x4.shape, x4.dtype),
        grid=(G, n_c),
        in_specs=[
            perm_spec, perm_spec, perm_spec, perm_spec, x_spec,
            pl.BlockSpec(gate_w.shape, c3, pipeline_mode=pl.Buffered(1)),
            pl.BlockSpec((1, LRU_WIDTH), c2),
            pl.BlockSpec((1, LRU_WIDTH), c2),
            pl.BlockSpec((1, LRU_WIDTH), c2),
            pl.BlockSpec((1, LRU_WIDTH), c2),
            pl.BlockSpec((D_MODEL, D_MODEL), c2, pipeline_mode=pl.Buffered(1)),
            pl.BlockSpec((1, D_MODEL), c2),
        ],
        out_specs=x_spec,
        scratch_shapes=[
            pltpu.VMEM((rows, LRU_WIDTH), _F32),
            pltpu.VMEM((rows, LRU_WIDTH), _F32),
            pltpu.VMEM((rows, LRU_WIDTH), _F32),
            pltpu.VMEM((SUBLANES, LRU_WIDTH), _F32),
            pltpu.VMEM((D_MODEL // LANES, rows, LANES), _F32),
        ],
        compiler_params=pltpu.CompilerParams(
            dimension_semantics=("arbitrary", "arbitrary"), vmem_limit_bytes=VMEM_LIMIT),
        name="mix_backward",
    )(xc_p, hf_p, gb_p, pool_p, x4, gate_w, b_a, b_x, lam, lru_g, w_out_b, post_g)


def _block_diag(w, per_block):
    n, k, _ = w.shape
    w = w.reshape(n // per_block, per_block, k, k)
    eye = jnp.eye(per_block, dtype=w.dtype)
    return jnp.einsum('ghij,hk->ghikj', w, eye).reshape(
        n // per_block, per_block * k, per_block * k)


def _pack_ffn(w_in, w_out):
    wg = w_in[:, :D_FF].reshape(D_MODEL, N_FF_CHUNKS, FF_CHUNK)
    wu = w_in[:, D_FF:].reshape(D_MODEL, N_FF_CHUNKS, FF_CHUNK)
    w_in_r = jnp.concatenate([wg, wu], axis=-1).transpose(1, 0, 2).astype(_BF16)
    w_out_r = w_out.reshape(N_FF_CHUNKS, FF_CHUNK, D_MODEL).astype(_BF16)
    return w_in_r, w_out_r


def _pack_gates(w_a, w_x):
    per = MXU_DIM // LRU_HEAD_DIM
    return jnp.concatenate([_block_diag(w_a, per), _block_diag(w_x, per)],
                           axis=-1).astype(_BF16)


def _layer(x, p):
    B, S, D = x.shape
    assert D == D_MODEL and B % SEQ_GROUP == 0
    x2 = _ffn(x.reshape(B * S, D), *p["ffn1"])
    x4 = x2.reshape(B // SEQ_GROUP, SEQ_GROUP, S, D)
    xc_p, hf_p, gb_p, pool_p = _mix1(x4, *p["mix1"])
    y4 = _mix2(xc_p, hf_p, gb_p, pool_p, x4, *p["mix2"])
    y2 = _ffn(y4.reshape(B * S, D), *p["ffn2"])
    return y2.reshape(B, S, D)


def kernel(x_prompt, x_sample, ffn1_pre_g, ffn1_post_g, ffn1_w_in, ffn1_w_out, mix_pre_g, mix_post_g, w_in, conv_w, conv_b, lru_w_a, lru_b_a, lru_w_x, lru_b_x, lru_lam, lru_out_g, pool_w, pool_scale, pool_out_g, w_out, ffn2_pre_g, ffn2_post_g, ffn2_w_in, ffn2_w_out):
    depth = ffn1_pre_g.shape[0]
    y_prompt, y_sample = x_prompt, x_sample
    for l in range(depth):
        pool_bd = _block_diag(pool_w[l], MXU_DIM // POOL_GROUP_DIM).astype(_BF16)
        p = {
            "ffn1": (ffn1_pre_g[l][None], ffn1_post_g[l][None],
                     *_pack_ffn(ffn1_w_in[l], ffn1_w_out[l])),
            "mix1": (mix_pre_g[l][None], w_in[l].astype(_BF16), conv_w[l], conv_b[l][None],
                     _pack_gates(lru_w_a[l, 0], lru_w_x[l, 0]), lru_b_a[l, 0][None],
                     lru_b_x[l, 0][None], lru_lam[l, 0][None], pool_bd,
                     pool_scale[l][None], pool_out_g[l][None]),
            "mix2": (_pack_gates(lru_w_a[l, 1], lru_w_x[l, 1]), lru_b_a[l, 1][None],
                     lru_b_x[l, 1][None], lru_lam[l, 1][None], lru_out_g[l][None],
                     w_out[l].astype(_BF16), mix_post_g[l][None]),
            "ffn2": (ffn2_pre_g[l][None], ffn2_post_g[l][None],
                     *_pack_ffn(ffn2_w_in[l], ffn2_w_out[l])),
        }
        y_prompt = _layer(y_prompt, p)
        y_sample = _layer(y_sample, p)
    return (y_prompt, y_sample)
```
